```python
import jax, jax.numpy as jnp
from jax import lax
import numpy as np

D_MODEL = 2048
BATCH = 2
SEQ = 16384
DEPTH = 4

CHUNK = 64
Q_BLOCK = 128
EPS = 1e-6

MLA_HEADS = 8
QK_NOPE = 128
QK_ROPE = 64
QK_DIM = QK_NOPE + QK_ROPE
V_DIM = 128
KV_RANK = 512
ROPE_THETA = 10000.0
MLA_WIDTH = MLA_HEADS * V_DIM

RWKV_HEADS = 8
RWKV_HEAD = 64
RWKV_WIDTH = RWKV_HEADS * RWKV_HEAD
DECAY_RANK = 64
A_RANK = 64
GATE_RANK = 128
GN_EPS = 64e-5

POOL_WINDOWS = (2, 4, 8, 16)
POOL_GROUPS = 4
POOL_WIDTH = 512
POOL_GROUP = POOL_WIDTH // POOL_GROUPS

MIX_WIDTH = MLA_WIDTH + RWKV_WIDTH + POOL_WIDTH

Q_COLS = MLA_HEADS * QK_DIM
RWKV_COLS = 3 * RWKV_WIDTH + DECAY_RANK + A_RANK + GATE_RANK
IN_COLS = Q_COLS + KV_RANK + QK_ROPE + RWKV_COLS + POOL_WIDTH

D_FF = 5632
N_EXPERTS = 8
TOP_K = 2
N_DENSE = (DEPTH + 1) // 2
N_MOE = DEPTH // 2

kernel_name = "hybrid_mla_rwkv7_pool_moe_trunk"


def rms_norm(x, g, eps=EPS):
    xf = x.astype(jnp.float32)
    y = xf * lax.rsqrt(jnp.mean(xf * xf, axis=-1, keepdims=True) + eps)
    return (y * g).astype(x.dtype)


def rope_tables(positions):
    inv_freq = 1.0 / (ROPE_THETA ** (jnp.arange(0, QK_ROPE, 2, dtype=jnp.float32) / QK_ROPE))
    ang = positions.astype(jnp.float32)[..., None] * inv_freq
    return jnp.cos(ang), jnp.sin(ang)


def apply_rope(x, cos, sin):
    half = x.shape[-1] // 2
    x1, x2 = x[..., :half], x[..., half:]
    c, s = cos[:, :, None, :], sin[:, :, None, :]
    return jnp.concatenate([x1 * c - x2 * s, x2 * c + x1 * s], axis=-1).astype(x.dtype)


def chunk_causal_attention(q, k, v, chunk_id):
    B, S, H, Dq = q.shape
    nqb = S // Q_BLOCK
    qb = q.reshape(B, nqb, Q_BLOCK, H, Dq).transpose(1, 0, 2, 3, 4)
    cqb = chunk_id.reshape(B, nqb, Q_BLOCK).transpose(1, 0, 2)
    scale = Dq ** -0.5

    def one_block(args):
        qi, ci = args
        s = jnp.einsum('bqhd,bkhd->bhqk', qi, k).astype(jnp.float32) * scale
        allowed = chunk_id[:, None, None, :] <= ci[:, None, :, None]
        s = jnp.where(allowed, s, -jnp.inf)
        p = jax.nn.softmax(s, axis=-1).astype(v.dtype)
        return jnp.einsum('bhqk,bkhd->bqhd', p, v)

    o = lax.map(one_block, (qb, cqb))
    return o.transpose(1, 0, 2, 3, 4).reshape(B, S, H, v.shape[-1])


def mla_group(u_q, u_ckv, u_kr, kv_norm, w_uk, w_uv, q_norm, k_norm, cos, sin, chunk_id):
    B, S, _ = u_q.shape
    q = rms_norm(u_q.reshape(B, S, MLA_HEADS, QK_DIM), q_norm)
    c_kv = rms_norm(u_ckv, kv_norm)
    k_nope = jnp.einsum('bsc,chd->bshd', c_kv, w_uk)
    v = jnp.einsum('bsc,chd->bshd', c_kv, w_uv)
    k_rope = jnp.broadcast_to(u_kr[:, :, None, :], (B, S, MLA_HEADS, QK_ROPE))
    k = rms_norm(jnp.concatenate([k_nope, k_rope], axis=-1), k_norm)
    q = jnp.concatenate([q[..., :QK_NOPE], apply_rope(q[..., QK_NOPE:], cos, sin)], axis=-1)
    k = jnp.concatenate([k[..., :QK_NOPE], apply_rope(k[..., QK_NOPE:], cos, sin)], axis=-1)
    o = chunk_causal_attention(q, k, v, chunk_id)
    return o.reshape(B, S, MLA_WIDTH)


def rwkv7_scan(r, w, k, v, kk, b):
    B, S, H, N = r.shape

    def step(state, inp):
        r_t, w_t, k_t, v_t, kk_t, b_t = inp
        sa = -jnp.einsum('bhvk,bhk->bhv', state, kk_t)
        state = (state * w_t[:, :, None, :]
                 + sa[..., None] * b_t[:, :, None, :]
                 + v_t[..., None] * k_t[:, :, None, :])
        return state, jnp.einsum('bhvk,bhk->bhv', state, r_t)

    xs = tuple(jnp.moveaxis(t, 1, 0) for t in (r, w, k, v, kk, b))
    state0 = jnp.zeros((B, H, N, N), jnp.float32)
    _, y = lax.scan(step, state0, xs)
    return jnp.moveaxis(y, 0, 1)


def rwkv7_time_mix(u, mu, w0, w_up, a0, a_up, g_up, k_k, k_a, r_k, ln_w, ln_b):
    B, S, _ = u.shape
    W = RWKV_WIDTH
    uf = u.astype(jnp.float32)
    prev = jnp.pad(uf, ((0, 0), (1, 0), (0, 0)))[:, :S]
    uf = uf + mu * (prev - uf)
    r, k, v, wd, ad, gd = jnp.split(
        uf, [W, 2 * W, 3 * W, 3 * W + DECAY_RANK, 3 * W + DECAY_RANK + A_RANK], axis=-1)
    w_log = -jax.nn.softplus(-(w0 + jnp.tanh(wd) @ w_up)) - 0.5
    decay = jnp.exp(-jnp.exp(w_log))
    a = jax.nn.sigmoid(a0 + ad @ a_up)
    g = jax.nn.sigmoid(gd) @ g_up
    heads = lambda t: t.reshape(B, S, RWKV_HEADS, RWKV_HEAD)
    kk = heads(k * k_k)
    kk = kk * lax.rsqrt(jnp.sum(kk * kk, axis=-1, keepdims=True) + 1e-12)
    k = k * (1.0 + (a - 1.0) * k_a)
    r, k, v, decay, a = heads(r), heads(k), heads(v), heads(decay), heads(a)
    y = rwkv7_scan(r, decay, k, v, kk, kk * a)
    mean = jnp.mean(y, axis=-1, keepdims=True)
    var = jnp.mean(jnp.square(y - mean), axis=-1, keepdims=True)
    y = ((y - mean) * lax.rsqrt(var + GN_EPS)).reshape(B, S, W) * ln_w + ln_b
    bonus = jnp.sum(r * k * r_k, axis=-1, keepdims=True) * v
    y = (y + bonus.reshape(B, S, W)) * g
    return y.astype(u.dtype)


def multiscale_pool(u, w_pool, scale):
    B, S, C = u.shape
    uf = u.astype(jnp.float32)
    cs = jnp.cumsum(uf, axis=1)
    t = jnp.arange(S)
    outs = []
    for gi, win in enumerate(POOL_WINDOWS):
        lo, hi = gi * POOL_GROUP, (gi + 1) * POOL_GROUP
        seg = cs[..., lo:hi]
        lag = jnp.pad(seg, ((0, 0), (win, 0), (0, 0)))[:, :S]
        cnt = jnp.minimum(t + 1, win).astype(jnp.float32)[None, :, None]
        outs.append((seg - lag) / cnt - uf[..., lo:hi])
    d = jnp.stack(outs, axis=2)
    y = jnp.einsum('bsgc,gcd->bsgd', d, w_pool).reshape(B, S, C)
    return (y * scale).astype(u.dtype)


def swiglu(h, w_gate, w_up, w_down):
    return (jax.nn.silu(h @ w_gate) * (h @ w_up)) @ w_down


def moe_swiglu(h, w_router, w_gate, w_up, w_down):
    logits = (h @ w_router).astype(jnp.float32)
    top_val, top_idx = lax.top_k(logits, TOP_K)
    top_w = jax.nn.softmax(top_val, axis=-1)
    combine = jnp.sum(jax.nn.one_hot(top_idx, N_EXPERTS, dtype=jnp.float32) * top_w[..., None], axis=-2)
    y = jnp.zeros_like(h)
    for e in range(N_EXPERTS):
        y = y + combine[..., e:e + 1].astype(h.dtype) * swiglu(h, w_gate[e], w_up[e], w_down[e])
    return y


def setup_inputs(seed: int = 0) -> dict:
    key = jax.random.key(seed)
    ks = iter(jax.random.split(key, 40))
    L = DEPTH
    resid = (2.0 * DEPTH) ** -0.5

    def nrm(shape, scale):
        return scale * jax.random.normal(next(ks), shape, jnp.float32)

    def gain(shape):
        return 1.0 + nrm(shape, 0.05)

    def unif(shape, lo, hi):
        return jax.random.uniform(next(ks), shape, jnp.float32, lo, hi)

    return {
        "x": nrm((BATCH, SEQ, D_MODEL), 1.0),
        "positions": jnp.broadcast_to(jnp.arange(SEQ, dtype=jnp.int32), (BATCH, SEQ)),
        "attn_norm": gain((L, D_MODEL)),
        "w_in": nrm((L, D_MODEL, IN_COLS), D_MODEL ** -0.5),
        "mla_kv_norm": gain((L, KV_RANK)),
        "mla_w_uk": nrm((L, KV_RANK, MLA_HEADS, QK_NOPE), KV_RANK ** -0.5),
        "mla_w_uv": nrm((L, KV_RANK, MLA_HEADS, V_DIM), KV_RANK ** -0.5),
        "mla_q_norm": gain((L, QK_DIM)),
        "mla_k_norm": gain((L, QK_DIM)),
        "rwkv_mu": unif((L, RWKV_COLS), 0.0, 1.0),
        "rwkv_w0": unif((L, RWKV_WIDTH), -6.0, -0.5),
        "rwkv_w_up": nrm((L, DECAY_RANK, RWKV_WIDTH), 0.1 * DECAY_RANK ** -0.5),
        "rwkv_a0": nrm((L, RWKV_WIDTH), 0.5),
        "rwkv_a_up": nrm((L, A_RANK, RWKV_WIDTH), 0.1 * A_RANK ** -0.5),
        "rwkv_g_up": nrm((L, GATE_RANK, RWKV_WIDTH), GATE_RANK ** -0.5),
        "rwkv_k_k": 0.85 + nrm((L, RWKV_WIDTH), 0.02),
        "rwkv_k_a": gain((L, RWKV_WIDTH)),
        "rwkv_r_k": nrm((L, RWKV_HEADS, RWKV_HEAD), 0.1),
        "rwkv_ln_w": gain((L, RWKV_WIDTH)),
        "rwkv_ln_b": nrm((L, RWKV_WIDTH), 0.02),
        "pool_w": nrm((L, POOL_GROUPS, POOL_GROUP, POOL_GROUP), POOL_GROUP ** -0.5),
        "pool_scale": gain((L, POOL_WIDTH)),
        "w_out": nrm((L, MIX_WIDTH, D_MODEL), resid * MIX_WIDTH ** -0.5),
        "ffn_norm": gain((L, D_MODEL)),
        "dense_w_gate": nrm((N_DENSE, D_MODEL, D_FF), D_MODEL ** -0.5),
        "dense_w_up": nrm((N_DENSE, D_MODEL, D_FF), D_MODEL ** -0.5),
        "dense_w_down": nrm((N_DENSE, D_FF, D_MODEL), resid * D_FF ** -0.5),
        "router_w": nrm((N_MOE, D_MODEL, N_EXPERTS), D_MODEL ** -0.5),
        "moe_w_gate": nrm((N_MOE, N_EXPERTS, D_MODEL, D_FF), D_MODEL ** -0.5),
        "moe_w_up": nrm((N_MOE, N_EXPERTS, D_MODEL, D_FF), D_MODEL ** -0.5),
        "moe_w_down": nrm((N_MOE, N_EXPERTS, D_FF, D_MODEL), resid * D_FF ** -0.5),
    }


def reference(x, positions, attn_norm, w_in, mla_kv_norm, mla_w_uk, mla_w_uv, mla_q_norm,
              mla_k_norm, rwkv_mu, rwkv_w0, rwkv_w_up, rwkv_a0, rwkv_a_up, rwkv_g_up,
              rwkv_k_k, rwkv_k_a, rwkv_r_k, rwkv_ln_w, rwkv_ln_b, pool_w, pool_scale,
              w_out, ffn_norm, dense_w_gate, dense_w_up, dense_w_down, router_w,
              moe_w_gate, moe_w_up, moe_w_down):
    cos, sin = rope_tables(positions)
    chunk_id = positions // CHUNK
    splits = [Q_COLS, Q_COLS + KV_RANK, Q_COLS + KV_RANK + QK_ROPE,
              Q_COLS + KV_RANK + QK_ROPE + RWKV_COLS]
    for l in range(DEPTH):
        h = rms_norm(x, attn_norm[l])
        u = h @ w_in[l]
        u_q, u_ckv, u_kr, u_rw, u_pool = jnp.split(u, splits, axis=-1)
        y_mla = mla_group(u_q, u_ckv, u_kr, mla_kv_norm[l], mla_w_uk[l], mla_w_uv[l],
                          mla_q_norm[l], mla_k_norm[l], cos, sin, chunk_id)
        y_rw = rwkv7_time_mix(u_rw, rwkv_mu[l], rwkv_w0[l], rwkv_w_up[l], rwkv_a0[l],
                              rwkv_a_up[l], rwkv_g_up[l], rwkv_k_k[l], rwkv_k_a[l],
                              rwkv_r_k[l], rwkv_ln_w[l], rwkv_ln_b[l])
        y_pool = multiscale_pool(u_pool, pool_w[l], pool_scale[l])
        x = x + jnp.concatenate([y_mla, y_rw, y_pool], axis=-1) @ w_out[l]
        h = rms_norm(x, ffn_norm[l])
        i = l // 2
        if l % 2 == 0:
            x = x + swiglu(h, dense_w_gate[i], dense_w_up[i], dense_w_down[i])
        else:
            x = x + moe_swiglu(h, router_w[i], moe_w_gate[i], moe_w_up[i], moe_w_down[i])
    return x
```

```python
import functools

import jax
import jax.numpy as jnp
from jax import lax
from jax.experimental import pallas as pl
from jax.experimental.pallas import tpu as pltpu

F32 = jnp.float32
BF16 = jnp.bfloat16

D_MODEL = 2048
DEPTH = 4
CHUNK = 64
EPS = 1e-6
MLA_HEADS = 8
QK_NOPE = 128
QK_ROPE = 64
QK_DIM = QK_NOPE + QK_ROPE
V_DIM = 128
KV_RANK = 512
ROPE_THETA = 10000.0
MLA_WIDTH = MLA_HEADS * V_DIM
RWKV_HEADS = 8
RWKV_HEAD = 64
RWKV_WIDTH = RWKV_HEADS * RWKV_HEAD
DECAY_RANK = 64
A_RANK = 64
GATE_RANK = 128
GN_EPS = 64e-5
POOL_WINDOWS = (2, 4, 8, 16)
POOL_GROUP = 128
POOL_WIDTH = 512
Q_COLS = MLA_HEADS * QK_DIM
RWKV_COLS = 3 * RWKV_WIDTH + DECAY_RANK + A_RANK + GATE_RANK
D_FF = 5632
N_EXPERTS = 8

LANES = 128
VMEM_LIMIT = 52 * 1024 * 1024

OFF_QN = 0
OFF_QR = OFF_QN + MLA_HEADS * QK_NOPE
OFF_CKV = OFF_QR + MLA_HEADS * QK_ROPE
OFF_KR = OFF_CKV + KV_RANK
OFF_R = OFF_KR + 256
OFF_K = OFF_R + RWKV_WIDTH
OFF_V = OFF_K + RWKV_WIDTH
OFF_S = OFF_V + RWKV_WIDTH
OFF_POOL = OFF_S + 256
U_COLS = OFF_POOL + POOL_WIDTH


def _cparams(sem):
    return pltpu.CompilerParams(dimension_semantics=sem, vmem_limit_bytes=VMEM_LIMIT)


def _bdot(a, b):
    return jnp.dot(a.astype(BF16), b.astype(BF16), preferred_element_type=F32)


def _dot_nt(a, b):
    return lax.dot_general(a.astype(BF16), b.astype(BF16), (((1,), (1,)), ((), ())),
                           preferred_element_type=F32)


def _dot_tn(a, b):
    return lax.dot_general(a.astype(BF16), b.astype(BF16), (((0,), (0,)), ((), ())),
                           preferred_element_type=F32)


def _norm_matmul_kernel(x_ref, g_ref, w_ref, o_ref, h_ref):
    @pl.when(pl.program_id(1) == 0)
    def _():
        x = x_ref[...]
        ms = jnp.mean(x * x, axis=-1, keepdims=True)
        h_ref[...] = (x * lax.rsqrt(ms + EPS) * g_ref[...]).astype(BF16)

    o_ref[...] = jnp.dot(h_ref[...], w_ref[...], preferred_element_type=F32).astype(o_ref.dtype)


def norm_matmul(x, g, w, tm, tn, out_dtype=F32):
    T, D = x.shape
    N = w.shape[1]
    return pl.pallas_call(
        _norm_matmul_kernel,
        grid=(T // tm, N // tn),
        in_specs=[pl.BlockSpec((tm, D), lambda i, j: (i, 0)),
                  pl.BlockSpec((1, D), lambda i, j: (0, 0)),
                  pl.BlockSpec((D, tn), lambda i, j: (0, j))],
        out_specs=pl.BlockSpec((tm, tn), lambda i, j: (i, j)),
        out_shape=jax.ShapeDtypeStruct((T, N), out_dtype),
        scratch_shapes=[pltpu.VMEM((tm, D), BF16)],
        compiler_params=_cparams(("parallel", "arbitrary")),
        name="norm_matmul",
    )(x, g, w)


def _rope_pairs(t, cos4, sin4, lane):
    rot = jnp.where((lane % 64) < 32, pltpu.roll(t, 96, 1), pltpu.roll(t, 32, 1))
    return t * cos4 + rot * sin4


def _mla_prep_kernel(qn_ref, qr_ref, ckv_ref, kr_ref, cos_ref, sin_ref, wuk_ref, wuv_ref,
                     kvn_ref, gq_ref, gk_ref, q_ref, k_ref, v_ref):
    tm = qn_ref.shape[0]
    lane = lax.broadcasted_iota(jnp.int32, (tm, LANES), 1)
    low = lane < 64
    cos4 = cos_ref[...]
    sin4 = sin_ref[...]
    scale = QK_DIM ** -0.5
    gq_nope = gq_ref[0:1, :]
    gq_rope = gq_ref[1:2, :]
    gk_nope = gk_ref[0:1, :]
    gk_rope = gk_ref[1:2, :]

    for hp in range(MLA_HEADS // 2):
        raw = qr_ref[:, hp * LANES:(hp + 1) * LANES]
        sq = raw * raw
        ss_pair = (jnp.sum(jnp.where(low, sq, 0.0), axis=-1, keepdims=True),
                   jnp.sum(jnp.where(low, 0.0, sq), axis=-1, keepdims=True))
        roped = _rope_pairs(raw * gq_rope, cos4, sin4, lane)
        for half in range(2):
            h = 2 * hp + half
            nope = qn_ref[:, h * LANES:(h + 1) * LANES]
            ss = jnp.sum(nope * nope, axis=-1, keepdims=True) + ss_pair[half]
            rinv = lax.rsqrt(ss * (1.0 / QK_DIM) + EPS) * scale
            rope_h = roped if half == 0 else pltpu.roll(roped, 64, 1)
            q_ref[0, h, :, 0:LANES] = (nope * gq_nope * rinv).astype(BF16)
            q_ref[0, h, :, LANES:2 * LANES] = jnp.where(low, rope_h * rinv, 0.0).astype(BF16)

    c = ckv_ref[...]
    ms = jnp.mean(c * c, axis=-1, keepdims=True)
    cn = (c * lax.rsqrt(ms + EPS) * kvn_ref[...]).astype(BF16)
    k_all = jnp.dot(cn, wuk_ref[...], preferred_element_type=F32)
    v_all = jnp.dot(cn, wuv_ref[...], preferred_element_type=F32)
    kr = kr_ref[...]
    ss_kr = jnp.sum(kr * kr, axis=-1, keepdims=True)
    kr_roped = jnp.where(low, _rope_pairs(kr * gk_rope, cos4, sin4, lane), 0.0)
    for h in range(MLA_HEADS):
        kn = k_all[:, h * LANES:(h + 1) * LANES]
        ss = jnp.sum(kn * kn, axis=-1, keepdims=True) + ss_kr
        rinv = lax.rsqrt(ss * (1.0 / QK_DIM) + EPS)
        k_ref[0, h, :, 0:LANES] = (kn * gk_nope * rinv).astype(BF16)
        k_ref[0, h, :, LANES:2 * LANES] = (kr_roped * rinv).astype(BF16)
        v_ref[0, h, :, :] = v_all[:, h * LANES:(h + 1) * LANES].astype(BF16)


def mla_prep(u, cos4, sin4, w_uk, w_uv, kv_norm, gq, gk, B, S, tm):
    T = B * S
    nt = S // tm
    H = MLA_HEADS
    col = lambda off, width: (lambda i: (i, off // width))
    whole = lambda shape: pl.BlockSpec(shape, lambda i: (0,) * len(shape))
    hspec = lambda d: pl.BlockSpec((1, H, tm, d), lambda i: (i // nt, 0, i % nt, 0))
    return pl.pallas_call(
        _mla_prep_kernel,
        grid=(T // tm,),
        in_specs=[pl.BlockSpec((tm, 1024), col(OFF_QN, 1024)),
                  pl.BlockSpec((tm, 512), col(OFF_QR, 512)),
                  pl.BlockSpec((tm, 512), col(OFF_CKV, 512)),
                  pl.BlockSpec((tm, LANES), col(OFF_KR, LANES)),
                  pl.BlockSpec((tm, LANES), lambda i: (i, 0)),
                  pl.BlockSpec((tm, LANES), lambda i: (i, 0)),
                  whole((KV_RANK, H * QK_NOPE)), whole((KV_RANK, H * V_DIM)),
                  whole((1, KV_RANK)), whole((2, LANES)), whole((2, LANES))],
        out_specs=[hspec(256), hspec(256), hspec(V_DIM)],
        out_shape=[jax.ShapeDtypeStruct((B, H, S, 256), BF16),
                   jax.ShapeDtypeStruct((B, H, S, 256), BF16),
                   jax.ShapeDtypeStruct((B, H, S, V_DIM), BF16)],
        compiler_params=_cparams(("parallel",)),
        name="mla_prep",
    )(u, u, u, u, cos4, sin4, w_uk, w_uv, kv_norm, gq, gk)


NEG_BIG = -1e30


def _attn_kernel(qlo_ref, qhi_ref, klo_ref, khi_ref, q_ref, k_ref, v_ref, cq_ref, ck_ref,
                 o_ref, m_ref, l_ref, acc_ref, *, tk):
    b = pl.program_id(0)
    i = pl.program_id(2)
    nk = k_ref.shape[2] // tk
    q = q_ref[0, 0]
    cq = cq_ref[0]
    q_lo = qlo_ref[b, i]
    q_hi = qhi_ref[b, i]
    m_ref[...] = jnp.full(m_ref.shape, NEG_BIG, F32)
    l_ref[...] = jnp.zeros(l_ref.shape, F32)
    acc_ref[...] = jnp.zeros(acc_ref.shape, F32)

    def update(j, masked):
        start = pl.multiple_of(j * tk, tk)
        kj = k_ref[0, 0, pl.ds(start, tk), :]
        vj = v_ref[0, 0, pl.ds(start, tk), :]
        s = _dot_nt(q, kj)
        if masked:
            ck = ck_ref[0, :, pl.ds(start, tk)]
            s = jnp.where(ck <= cq, s, -jnp.inf)
        m_old = m_ref[...]
        m_new = jnp.maximum(m_old, jnp.max(s, axis=-1, keepdims=True))
        alpha = jnp.exp(m_old - m_new)
        p = jnp.exp(s - m_new)
        l_ref[...] = alpha * l_ref[...] + jnp.sum(p, axis=-1, keepdims=True)
        acc_ref[...] = alpha * acc_ref[...] + jnp.dot(p.astype(BF16), vj,
                                                      preferred_element_type=F32)
        m_ref[...] = m_new

    def step(j, carry):
        k_lo = klo_ref[b, j]
        k_hi = khi_ref[b, j]
        visible = k_lo <= q_hi
        full = k_hi <= q_lo

        @pl.when(jnp.logical_and(visible, full))
        def _():
            update(j, False)

        @pl.when(jnp.logical_and(visible, jnp.logical_not(full)))
        def _():
            update(j, True)

        return carry

    lax.fori_loop(0, nk, step, 0)
    o_ref[0] = (acc_ref[...] * (1.0 / l_ref[...])).astype(o_ref.dtype)


def attention(q, k, v, chunk_id, tq, tk):
    B, H, S, DQ = q.shape
    DV = v.shape[-1]
    nq, nk = S // tq, S // tk
    cq = chunk_id.reshape(B, S, 1)
    ck = chunk_id.reshape(B, 1, S)
    qb = chunk_id.reshape(B, nq, tq)
    kb = chunk_id.reshape(B, nk, tk)
    grid_spec = pltpu.PrefetchScalarGridSpec(
        num_scalar_prefetch=4,
        grid=(B, H, nq),
        in_specs=[pl.BlockSpec((1, 1, tq, DQ), lambda b, h, i, *_: (b, h, i, 0)),
                  pl.BlockSpec((1, 1, S, DQ), lambda b, h, i, *_: (b, h, 0, 0)),
                  pl.BlockSpec((1, 1, S, DV), lambda b, h, i, *_: (b, h, 0, 0)),
                  pl.BlockSpec((1, tq, 1), lambda b, h, i, *_: (b, i, 0)),
                  pl.BlockSpec((1, 1, S), lambda b, h, i, *_: (b, 0, 0))],
        out_specs=pl.BlockSpec((1, tq, DV), lambda b, h, i, *_: (b, i, h)),
        scratch_shapes=[pltpu.VMEM((tq, 1), F32), pltpu.VMEM((tq, 1), F32),
                        pltpu.VMEM((tq, DV), F32)],
    )
    return pl.pallas_call(
        functools.partial(_attn_kernel, tk=tk),
        grid_spec=grid_spec,
        out_shape=jax.ShapeDtypeStruct((B, S, H * DV), BF16),
        compiler_params=_cparams(("parallel", "parallel", "arbitrary")),
        name="attention",
    )(qb.min(-1), qb.max(-1), kb.min(-1), kb.max(-1), q, k, v, cq, ck)


P_MU_R, P_MU_K, P_MU_V, P_W0, P_A0, P_KK, P_KA, P_LNW, P_LNB, P_RK = range(10)
P_ROWS = 16
RC = 64


def _half_sums(x, low):
    s0 = jnp.sum(jnp.where(low, x, 0.0), axis=-1, keepdims=True)
    s1 = jnp.sum(jnp.where(low, 0.0, x), axis=-1, keepdims=True)
    return jnp.where(low, s0, s1)


def _rwkv_kernel(r_ref, k_ref, v_ref, s_ref, p_ref, mus_ref, wup_ref, aup_ref, gup_ref, o_ref,
                 h_ref, pr_ref, pk_ref, pv_ref, ps_ref, y_ref):
    ct = r_ref.shape[1]
    t = pl.program_id(2)

    @pl.when(t == 0)
    def _():
        h_ref[...] = jnp.zeros(h_ref.shape, F32)
        pr_ref[...] = jnp.zeros(pr_ref.shape, F32)
        pk_ref[...] = jnp.zeros(pk_ref.shape, F32)
        pv_ref[...] = jnp.zeros(pv_ref.shape, F32)
        ps_ref[...] = jnp.zeros(ps_ref.shape, F32)

    def mixed(x, prev_ref, mu):
        first = lax.broadcasted_iota(jnp.int32, x.shape, 0) == 0
        shifted = jnp.where(first, prev_ref[0:1, :], pltpu.roll(x, 1, 0))
        prev_ref[0:1, :] = x[ct - 1:ct, :]
        return x + mu * (shifted - x)

    par = lambda row: p_ref[row:row + 1, :]
    r = mixed(r_ref[0], pr_ref, par(P_MU_R))
    k = mixed(k_ref[0], pk_ref, par(P_MU_K))
    v = mixed(v_ref[0], pv_ref, par(P_MU_V))
    sm = mixed(s_ref[0], ps_ref, mus_ref[...])
    wd = sm[:, 0:DECAY_RANK]
    ad = sm[:, DECAY_RANK:DECAY_RANK + A_RANK]
    gd = sm[:, DECAY_RANK + A_RANK:]

    lane = lax.broadcasted_iota(jnp.int32, (ct, LANES), 1)
    low = lane < RWKV_HEAD

    z = par(P_W0) + _bdot(jnp.tanh(wd), wup_ref[...])
    softplus = jnp.maximum(-z, 0.0) + jnp.log(1.0 + jnp.exp(-jnp.abs(z)))
    lw = -jnp.exp(-softplus - 0.5)
    a = 1.0 / (1.0 + jnp.exp(-(par(P_A0) + _bdot(ad, aup_ref[...]))))
    g = _bdot(1.0 / (1.0 + jnp.exp(-gd)), gup_ref[...])
    kk = k * par(P_KK)
    kk = kk * lax.rsqrt(_half_sums(kk * kk, low) + 1e-12)
    k = k * (1.0 + (a - 1.0) * par(P_KA))
    bb = kk * a

    row = lax.broadcasted_iota(jnp.int32, (ct, ct), 0)
    colm = lax.broadcasted_iota(jnp.int32, (ct, ct), 1)
    tri = jnp.where(jnp.logical_and(row // RC == colm // RC, colm <= row), 1.0, 0.0).astype(BF16)
    lw_hi = lw.astype(BF16)
    lw_lo = (lw - lw_hi.astype(F32)).astype(BF16)
    cum = (jnp.dot(tri, lw_hi, preferred_element_type=F32)
           + jnp.dot(tri, lw_lo, preferred_element_type=F32))

    r_t = r * jnp.exp(cum)
    a_t = -kk * jnp.exp(cum - lw)
    k_t = k * jnp.exp(-cum)
    b_t = bb * jnp.exp(-cum)

    ri = lax.broadcasted_iota(jnp.int32, (RC, RC), 0)
    ci = lax.broadcasted_iota(jnp.int32, (RC, RC), 1)
    strict = ci < ri
    incl = ci <= ri
    eye = ci == ri

    for c in range(ct // RC):
        rows = slice(c * RC, (c + 1) * RC)
        cum_c = cum[rows]
        to_end = jnp.exp(cum_c[RC - 1:RC, :] - cum_c)
        gamma_end = jnp.exp(cum_c[RC - 1:RC, :])
        b_end = bb[rows] * to_end
        k_end = k[rows] * to_end
        for hh in range(2):
            ln = slice(hh * RWKV_HEAD, (hh + 1) * RWKV_HEAD)
            at, rt, kt, bt = a_t[rows, ln], r_t[rows, ln], k_t[rows, ln], b_t[rows, ln]
            vv = v[rows, ln]
            a_ab = jnp.where(strict, _dot_nt(at, bt), 0.0)
            a_ak = jnp.where(strict, _dot_nt(at, kt), 0.0)
            a_rb = jnp.where(incl, _dot_nt(rt, bt), 0.0)
            a_rk = jnp.where(incl, _dot_nt(rt, kt), 0.0)
            inv = jnp.where(eye, 1.0, a_ab)
            pw = a_ab
            for _ in range(5):
                pw = _bdot(pw, pw)
                inv = inv + _bdot(inv, pw)
            w1 = _bdot(inv, _bdot(a_ak, vv))
            ta = _bdot(inv, at)
            q_eff = rt + _bdot(a_rb, ta)
            y_in = _bdot(a_rb, w1) + _bdot(a_rk, vv)
            be, ke = b_end[:, ln], k_end[:, ln]
            trans = jnp.where(eye, gamma_end[:, ln], 0.0) + _dot_tn(be, ta)
            inject = _dot_tn(be, w1) + _dot_tn(ke, vv)
            h = h_ref[hh]
            y_ref[rows, ln] = _bdot(q_eff, h) + y_in
            h_ref[hh] = _bdot(trans, h) + inject

    y = y_ref[...]
    inv_n = 1.0 / RWKV_HEAD
    mean = _half_sums(y, low) * inv_n
    yc = y - mean
    var = _half_sums(yc * yc, low) * inv_n
    yn = yc * lax.rsqrt(var + GN_EPS) * par(P_LNW) + par(P_LNB)
    bonus = _half_sums(r * k * par(P_RK), low) * v
    o_ref[0] = ((yn + bonus) * g).astype(o_ref.dtype)


def rwkv(u3, ptab, mu_s, w_up, a_up, g_up, ct):
    B, S, _ = u3.shape
    nh = RWKV_WIDTH // LANES
    ucol = lambda off: pl.BlockSpec((1, ct, LANES), lambda b, hp, t: (b, t, off // LANES + hp))
    return pl.pallas_call(
        _rwkv_kernel,
        grid=(B, nh, S // ct),
        in_specs=[ucol(OFF_R), ucol(OFF_K), ucol(OFF_V),
                  pl.BlockSpec((1, ct, 256), lambda b, hp, t: (b, t, OFF_S // 256)),
                  pl.BlockSpec((P_ROWS, LANES), lambda b, hp, t: (0, hp)),
                  pl.BlockSpec((1, 256), lambda b, hp, t: (0, 0)),
                  pl.BlockSpec((DECAY_RANK, LANES), lambda b, hp, t: (0, hp)),
                  pl.BlockSpec((A_RANK, LANES), lambda b, hp, t: (0, hp)),
                  pl.BlockSpec((GATE_RANK, LANES), lambda b, hp, t: (0, hp))],
        out_specs=pl.BlockSpec((1, ct, LANES), lambda b, hp, t: (b, t, hp)),
        out_shape=jax.ShapeDtypeStruct((B, S, RWKV_WIDTH), BF16),
        scratch_shapes=[pltpu.VMEM((2, RWKV_HEAD, RWKV_HEAD), F32),
                        pltpu.VMEM((8, LANES), F32), pltpu.VMEM((8, LANES), F32),
                        pltpu.VMEM((8, LANES), F32), pltpu.VMEM((8, 256), F32),
                        pltpu.VMEM((ct, LANES), F32)],
        compiler_params=_cparams(("parallel", "parallel", "arbitrary")),
        name="rwkv",
    )(u3, u3, u3, u3, ptab, mu_s, w_up, a_up, g_up)


HALO = 16


def _pool_kernel(u_ref, w_ref, sc_ref, o_ref, halo_ref):
    tm = u_ref.shape[1]
    t = pl.program_id(1)

    @pl.when(t == 0)
    def _():
        halo_ref[...] = jnp.zeros(halo_ref.shape, F32)

    x = u_ref[0]
    ext = jnp.concatenate([halo_ref[...], x], axis=0)
    halo_ref[...] = x[tm - HALO:tm, :]
    pos = t * tm + lax.broadcasted_iota(jnp.int32, (tm, POOL_GROUP), 0)
    for gi, win in enumerate(POOL_WINDOWS):
        cols = slice(gi * POOL_GROUP, (gi + 1) * POOL_GROUP)
        acc = ext[:, cols]
        span = 1
        while span < win:
            acc = acc + pltpu.roll(acc, span, 0)
            span *= 2
        cnt = jnp.minimum(pos + 1, win).astype(F32)
        d = acc[HALO:, :] / cnt - x[:, cols]
        y = _bdot(d, w_ref[gi])
        o_ref[0, :, cols] = (y * sc_ref[:, cols]).astype(o_ref.dtype)


def pool(u3, w_pool, scale, tm):
    B, S, _ = u3.shape
    return pl.pallas_call(
        _pool_kernel,
        grid=(B, S // tm),
        in_specs=[pl.BlockSpec((1, tm, POOL_WIDTH), lambda b, t: (b, t, OFF_POOL // POOL_WIDTH)),
                  pl.BlockSpec((4, POOL_GROUP, POOL_GROUP), lambda b, t: (0, 0, 0)),
                  pl.BlockSpec((1, POOL_WIDTH), lambda b, t: (0, 0))],
        out_specs=pl.BlockSpec((1, tm, POOL_WIDTH), lambda b, t: (b, t, 0)),
        out_shape=jax.ShapeDtypeStruct((B, S, POOL_WIDTH), BF16),
        scratch_shapes=[pltpu.VMEM((HALO, POOL_WIDTH), F32)],
        compiler_params=_cparams(("parallel", "arbitrary")),
        name="pool",
    )(u3, w_pool, scale)


def _out_proj_kernel(x_ref, ya_ref, yr_ref, yp_ref, wa_ref, wr_ref, wp_ref, o_ref):
    acc = jnp.dot(ya_ref[...], wa_ref[...], preferred_element_type=F32)
    acc += jnp.dot(yr_ref[...], wr_ref[...], preferred_element_type=F32)
    acc += jnp.dot(yp_ref[...], wp_ref[...], preferred_element_type=F32)
    o_ref[...] = x_ref[...] + acc


def out_proj(x, ya, yr, yp, w_out, tm):
    T, D = x.shape
    wa, wr, wp = (w_out[:MLA_WIDTH], w_out[MLA_WIDTH:MLA_WIDTH + RWKV_WIDTH],
                  w_out[MLA_WIDTH + RWKV_WIDTH:])
    rowblk = lambda n: pl.BlockSpec((tm, n), lambda i: (i, 0))
    whole = lambda a: pl.BlockSpec(a.shape, lambda i: (0, 0))
    return pl.pallas_call(
        _out_proj_kernel,
        grid=(T // tm,),
        in_specs=[rowblk(D), rowblk(MLA_WIDTH), rowblk(RWKV_WIDTH), rowblk(POOL_WIDTH),
                  whole(wa), whole(wr), whole(wp)],
        out_specs=rowblk(D),
        out_shape=jax.ShapeDtypeStruct((T, D), F32),
        compiler_params=_cparams(("parallel",)),
        name="out_proj",
    )(x, ya, yr, yp, wa, wr, wp)


def _rms(x, g):
    ms = jnp.mean(x * x, axis=-1, keepdims=True)
    return x * lax.rsqrt(ms + EPS) * g


def _ffn_kernel(x_ref, g_ref, wg_ref, wu_ref, wd_ref, o_ref, h_ref, acc_ref):
    f = pl.program_id(1)

    @pl.when(f == 0)
    def _():
        h_ref[...] = _rms(x_ref[...], g_ref[...]).astype(BF16)
        acc_ref[...] = jnp.zeros(acc_ref.shape, F32)

    h = h_ref[...]
    gate = jnp.dot(h, wg_ref[...], preferred_element_type=F32)
    up = jnp.dot(h, wu_ref[...], preferred_element_type=F32)
    act = gate * (1.0 / (1.0 + jnp.exp(-gate))) * up
    acc_ref[...] += jnp.dot(act.astype(BF16), wd_ref[...], preferred_element_type=F32)

    @pl.when(f == pl.num_programs(1) - 1)
    def _():
        o_ref[...] = x_ref[...] + acc_ref[...]


def ffn(x, g, wg, wu, wd, tm, tf):
    T, D = x.shape
    F = wg.shape[1]
    return pl.pallas_call(
        _ffn_kernel,
        grid=(T // tm, F // tf),
        in_specs=[pl.BlockSpec((tm, D), lambda i, f: (i, 0)),
                  pl.BlockSpec((1, D), lambda i, f: (0, 0)),
                  pl.BlockSpec((D, tf), lambda i, f: (0, f)),
                  pl.BlockSpec((D, tf), lambda i, f: (0, f)),
                  pl.BlockSpec((tf, D), lambda i, f: (f, 0))],
        out_specs=pl.BlockSpec((tm, D), lambda i, f: (i, 0)),
        out_shape=jax.ShapeDtypeStruct((T, D), F32),
        scratch_shapes=[pltpu.VMEM((tm, D), BF16), pltpu.VMEM((tm, D), F32)],
        compiler_params=_cparams(("parallel", "arbitrary")),
        name="ffn",
    )(x, g, wg, wu, wd)


def _moe_kernel(x_ref, g_ref, wrh_ref, wrl_ref, wg_ref, wu_ref, wd_ref, o_ref,
                h_ref, acc_ref, comb_ref):
    e = pl.program_id(1)
    f = pl.program_id(2)
    tm = x_ref.shape[0]
    lane = lax.broadcasted_iota(jnp.int32, (tm, LANES), 1)

    @pl.when(jnp.logical_and(e == 0, f == 0))
    def _():
        hf = _rms(x_ref[...], g_ref[...])
        hb = hf.astype(BF16)
        h_ref[...] = hb
        acc_ref[...] = jnp.zeros(acc_ref.shape, F32)
        hl = (hf - hb.astype(F32)).astype(BF16)
        logits = (jnp.dot(hb, wrh_ref[...], preferred_element_type=F32)
                  + jnp.dot(hb, wrl_ref[...], preferred_element_type=F32)
                  + jnp.dot(hl, wrh_ref[...], preferred_element_type=F32))
        logits = jnp.where(lane < N_EXPERTS, logits, -jnp.inf)
        top1 = jnp.max(logits, axis=-1, keepdims=True)
        idx1 = jnp.min(jnp.where(logits == top1, lane, LANES), axis=-1, keepdims=True)
        rest = jnp.where(lane == idx1, -jnp.inf, logits)
        top2 = jnp.max(rest, axis=-1, keepdims=True)
        idx2 = jnp.min(jnp.where(rest == top2, lane, LANES), axis=-1, keepdims=True)
        e2 = jnp.exp(top2 - top1)
        w1 = 1.0 / (1.0 + e2)
        w2 = e2 / (1.0 + e2)
        comb_ref[...] = jnp.where(lane == idx1, w1, jnp.where(lane == idx2, w2, 0.0))

    h = h_ref[...]
    ce = jnp.sum(jnp.where(lane == e, comb_ref[...], 0.0), axis=-1, keepdims=True)
    gate = jnp.dot(h, wg_ref[0], preferred_element_type=F32)
    up = jnp.dot(h, wu_ref[0], preferred_element_type=F32)
    act = gate * (1.0 / (1.0 + jnp.exp(-gate))) * up * ce
    acc_ref[...] += jnp.dot(act.astype(BF16), wd_ref[0], preferred_element_type=F32)

    @pl.when(jnp.logical_and(e == pl.num_programs(1) - 1, f == pl.num_programs(2) - 1))
    def _():
        o_ref[...] = x_ref[...] + acc_ref[...]


def moe(x, g, wr_hi, wr_lo, wg, wu, wd, tm, tf):
    T, D = x.shape
    E, _, F = wg.shape
    return pl.pallas_call(
        _moe_kernel,
        grid=(T // tm, E, F // tf),
        in_specs=[pl.BlockSpec((tm, D), lambda i, e, f: (i, 0)),
                  pl.BlockSpec((1, D), lambda i, e, f: (0, 0)),
                  pl.BlockSpec((D, LANES), lambda i, e, f: (0, 0)),
                  pl.BlockSpec((D, LANES), lambda i, e, f: (0, 0)),
                  pl.BlockSpec((1, D, tf), lambda i, e, f: (e, 0, f)),
                  pl.BlockSpec((1, D, tf), lambda i, e, f: (e, 0, f)),
                  pl.BlockSpec((1, tf, D), lambda i, e, f: (e, f, 0))],
        out_specs=pl.BlockSpec((tm, D), lambda i, e, f: (i, 0)),
        out_shape=jax.ShapeDtypeStruct((T, D), F32),
        scratch_shapes=[pltpu.VMEM((tm, D), BF16), pltpu.VMEM((tm, D), F32),
                        pltpu.VMEM((tm, LANES), F32)],
        compiler_params=_cparams(("parallel", "arbitrary", "arbitrary")),
        name="moe",
    )(x, g, wr_hi, wr_lo, wg, wu, wd)


def _arrange_w_in(w_in):
    L, D, _ = w_in.shape
    wq = w_in[:, :, :Q_COLS].reshape(L, D, MLA_HEADS, QK_DIM)
    q_nope = wq[..., :QK_NOPE].reshape(L, D, MLA_HEADS * QK_NOPE)
    q_rope = wq[..., QK_NOPE:].reshape(L, D, MLA_HEADS * QK_ROPE)
    o = Q_COLS
    ckv = w_in[:, :, o:o + KV_RANK]
    o += KV_RANK
    kr = w_in[:, :, o:o + QK_ROPE]
    o += QK_ROPE
    rw = w_in[:, :, o:o + RWKV_COLS]
    o += RWKV_COLS
    pw = w_in[:, :, o:]
    pad = jnp.zeros((L, D, 256 - QK_ROPE), w_in.dtype)
    return jnp.concatenate([q_nope, q_rope, ckv, kr, pad, rw, pw], axis=-1).astype(BF16)


def _rope_tables(positions):
    inv_freq = 1.0 / (ROPE_THETA ** (jnp.arange(0, QK_ROPE, 2, dtype=F32) / QK_ROPE))
    ang = positions.astype(F32).reshape(-1)[:, None] * inv_freq
    c, s = jnp.cos(ang), jnp.sin(ang)
    return jnp.concatenate([c, c, c, c], axis=-1), jnp.concatenate([-s, s, -s, s], axis=-1)


def _pad_lanes(v, width=LANES):
    return jnp.concatenate([v, jnp.zeros(v.shape[:-1] + (width - v.shape[-1],), v.dtype)], axis=-1)


def kernel(x, positions, attn_norm, w_in, mla_kv_norm, mla_w_uk, mla_w_uv, mla_q_norm, mla_k_norm, rwkv_mu, rwkv_w0, rwkv_w_up, rwkv_a0, rwkv_a_up, rwkv_g_up, rwkv_k_k, rwkv_k_a, rwkv_r_k, rwkv_ln_w, rwkv_ln_b, pool_w, pool_scale, w_out, ffn_norm, dense_w_gate, dense_w_up, dense_w_down, router_w, moe_w_gate, moe_w_up, moe_w_down):
    B, S, D = x.shape
    T = B * S
    L = w_in.shape[0]
    tm_proj = min(1024, S)
    tm_mix = min(512, S)
    tq = min(512, S)
    ct = min(256, S)
    tf = 512

    cos4, sin4 = _rope_tables(positions)
    chunk_id = positions // CHUNK
    w_in_a = _arrange_w_in(w_in)
    w_uk = mla_w_uk.reshape(L, KV_RANK, MLA_HEADS * QK_NOPE).astype(BF16)
    w_uv = mla_w_uv.reshape(L, KV_RANK, MLA_HEADS * V_DIM).astype(BF16)
    w_out_b = w_out.astype(BF16)
    pool_w_b = pool_w.astype(BF16)

    xf = x.reshape(T, D)
    for l in range(L):
        u = norm_matmul(xf, attn_norm[l][None], w_in_a[l], tm_proj, 512)
        gq = jnp.stack([mla_q_norm[l, :QK_NOPE], jnp.tile(mla_q_norm[l, QK_NOPE:], 2)])
        gk = jnp.stack([mla_k_norm[l, :QK_NOPE], _pad_lanes(mla_k_norm[l, QK_NOPE:])])
        q, k, v = mla_prep(u, cos4, sin4, w_uk[l], w_uv[l], mla_kv_norm[l][None], gq, gk,
                           B, S, tm_mix)
        y_mla = attention(q, k, v, chunk_id, tq, tq).reshape(T, MLA_WIDTH)
        u3 = u.reshape(B, S, U_COLS)
        mu = rwkv_mu[l]
        W = RWKV_WIDTH
        rows = [mu[:W], mu[W:2 * W], mu[2 * W:3 * W], rwkv_w0[l], rwkv_a0[l], rwkv_k_k[l],
                rwkv_k_a[l], rwkv_ln_w[l], rwkv_ln_b[l], rwkv_r_k[l].reshape(W)]
        ptab = jnp.concatenate([jnp.stack(rows), jnp.zeros((P_ROWS - len(rows), W), F32)])
        y_rw = rwkv(u3, ptab, mu[3 * W:][None], rwkv_w_up[l].astype(BF16),
                    rwkv_a_up[l].astype(BF16), rwkv_g_up[l].astype(BF16), ct).reshape(T, W)
        y_pool = pool(u3, pool_w_b[l], pool_scale[l][None], tm_mix).reshape(T, POOL_WIDTH)
        xf = out_proj(xf, y_mla, y_rw, y_pool, w_out_b[l], tm_mix)
        i = l // 2
        if l % 2 == 0:
            xf = ffn(xf, ffn_norm[l][None], dense_w_gate[i].astype(BF16),
                     dense_w_up[i].astype(BF16), dense_w_down[i].astype(BF16), tm_mix, tf)
        else:
            wr = _pad_lanes(router_w[i])
            wr_hi = wr.astype(BF16)
            wr_lo = (wr - wr_hi.astype(F32)).astype(BF16)
            xf = moe(xf, ffn_norm[l][None], wr_hi, wr_lo, moe_w_gate[i].astype(BF16),
                     moe_w_up[i].astype(BF16), moe_w_down[i].astype(BF16), tm_mix, tf)
    return xf.reshape(B, S, D)
```
